```python
import jax, jax.numpy as jnp
from jax import lax
import numpy as np

D_MODEL = 2048
BATCH = 1
SEQ = 8192
DEPTH = 1
DEC_BATCH = 128
DEC_SEQ = 8
PAST_LEN = 16384
PAGE_SIZE = 128

GLA_HEADS = 4
GLA_DK = D_MODEL // (2 * GLA_HEADS)
GLA_DV = D_MODEL // GLA_HEADS
GLA_RANK = 16
GLA_GATE_NORM = 16.0
GLA_CHUNK = 64
SWA_HEAD_DIM = 64
SWA_HEADS = D_MODEL // SWA_HEAD_DIM
SWA_KV_HEADS = SWA_HEADS // 4
SWA_GROUP = SWA_HEADS // SWA_KV_HEADS
WINDOW = 128
D_FF = 256 * ((8 * D_MODEL // 3 + 255) // 256)
ALPHA = (2.0 * DEPTH) ** 0.25
BETA = (8.0 * DEPTH) ** -0.25
LN_EPS = 1e-5
RMS_EPS = 1e-6
NEG_INF = -1e30

SPLIT_SIZES = (GLA_HEADS * GLA_DK, GLA_HEADS * GLA_DK, GLA_HEADS * GLA_DV, GLA_HEADS * GLA_DV, GLA_RANK,
               SWA_HEADS * SWA_HEAD_DIM, SWA_KV_HEADS * SWA_HEAD_DIM, SWA_KV_HEADS * SWA_HEAD_DIM,
               D_MODEL, D_MODEL)
D_IN = sum(SPLIT_SIZES)

kernel_name = 'hybrid_gla_swa_sink_macaron_deepnorm_step'


def layer_norm(x, w, b):
    xf = x.astype(jnp.float32)
    mu = jnp.mean(xf, axis=-1, keepdims=True)
    var = jnp.mean(jnp.square(xf - mu), axis=-1, keepdims=True)
    return ((xf - mu) * lax.rsqrt(var + LN_EPS) * w.astype(jnp.float32) + b.astype(jnp.float32)).astype(x.dtype)


def swiglu_ffn(x, w_in, w_out):
    gate, up = jnp.split(x @ w_in, 2, axis=-1)
    return (jax.nn.silu(gate) * up) @ w_out


def alibi_slopes():
    return 2.0 ** (-8.0 * jnp.arange(1, SWA_HEADS + 1, dtype=jnp.float32) / SWA_HEADS)


def gla_chunked(q, k, v, log_a, s0):
    B, T = q.shape[:2]
    C = GLA_CHUNK if T % GLA_CHUNK == 0 else T
    n = T // C

    def to_chunks(a):
        return a.reshape(B, n, C, a.shape[2], a.shape[3]).transpose(1, 0, 3, 2, 4)

    qc, kc, vc, gc = to_chunks(q), to_chunks(k), to_chunks(v), to_chunks(log_a)
    causal = jnp.tril(jnp.ones((C, C), dtype=bool))[:, :, None]

    def step(S, inp):
        qb, kb, vb, gb = inp
        qf, kf, vf = qb.astype(jnp.float32), kb.astype(jnp.float32), vb.astype(jnp.float32)
        b = jnp.cumsum(gb.astype(jnp.float32), axis=2)
        inter = jnp.einsum('bhtk,bhkv->bhtv', qf * jnp.exp(b), S)
        rel = jnp.where(causal, b[:, :, :, None, :] - b[:, :, None, :, :], -jnp.inf)
        scores = jnp.einsum('bhtk,bhsk,bhtsk->bhts', qf, kf, jnp.exp(rel))
        intra = jnp.einsum('bhts,bhsv->bhtv', scores, vf)
        bl = b[:, :, -1:, :]
        S_new = jnp.exp(bl[:, :, 0, :, None]) * S + jnp.einsum('bhsk,bhsv->bhkv', kf * jnp.exp(bl - b), vf)
        return S_new, inter + intra

    S_fin, o = lax.scan(step, s0.astype(jnp.float32), (qc, kc, vc, gc))
    o = o.transpose(1, 0, 3, 2, 4).reshape(B, T, q.shape[2], v.shape[3])
    return o, S_fin


def swa_block_attend(q, k, v, q_pos, k_pos, sinks):
    s = jnp.einsum('bnqhgd,bnkhd->bnhgqk', q, k).astype(jnp.float32) * (SWA_HEAD_DIM ** -0.5)
    dist = q_pos[:, :, None] - k_pos[:, None, :]
    valid = (dist >= 0) & (dist < WINDOW) & (k_pos[:, None, :] >= 0)
    slopes = alibi_slopes().reshape(SWA_KV_HEADS, SWA_GROUP, 1, 1)
    s = s - slopes * dist.astype(jnp.float32)[:, None, None]
    s = jnp.where(valid[:, None, None], s, NEG_INF)
    sink = jnp.broadcast_to(sinks.astype(jnp.float32).reshape(SWA_KV_HEADS, SWA_GROUP, 1, 1), s.shape[:-1] + (1,))
    p = jax.nn.softmax(jnp.concatenate([s, sink], axis=-1), axis=-1)[..., :-1]
    return jnp.einsum('bnhgqk,bnkhd->bnqhgd', p.astype(v.dtype), v)


def token_mixers(h, gla_s0, k_buf, v_buf, w_in, b_in, w_gla_gate, b_gla_gate, gla_norm_w, swa_sinks,
                 w_br_gla, w_br_swa, w_o):
    B, T, _ = h.shape
    z = h @ w_in + b_in
    split_idx = np.cumsum(SPLIT_SIZES)[:-1].tolist()
    q_g, k_g, v_g, g_g, r_g, q_s, k_s, v_s, gate_gla, gate_swa = jnp.split(z, split_idx, axis=-1)

    q_g = q_g.reshape(B, T, GLA_HEADS, GLA_DK) * (GLA_DK ** -0.5)
    k_g = k_g.reshape(B, T, GLA_HEADS, GLA_DK)
    v_g = v_g.reshape(B, T, GLA_HEADS, GLA_DV)
    log_a = jax.nn.log_sigmoid((r_g @ w_gla_gate + b_gla_gate).astype(jnp.float32)) / GLA_GATE_NORM
    log_a = log_a.reshape(B, T, GLA_HEADS, GLA_DK)
    s0 = jnp.zeros((B, GLA_HEADS, GLA_DK, GLA_DV), jnp.float32) if gla_s0 is None else gla_s0
    o_g, gla_state = gla_chunked(q_g, k_g, v_g, log_a, s0)
    of = o_g.astype(jnp.float32)
    of = of * lax.rsqrt(jnp.mean(jnp.square(of), axis=-1, keepdims=True) + RMS_EPS) * gla_norm_w.astype(jnp.float32)
    o_g = of.reshape(B, T, GLA_HEADS * GLA_DV).astype(h.dtype) * jax.nn.silu(g_g)
    u_gla = o_g @ w_br_gla
    gla_state = gla_state.astype(s0.dtype)

    q_s = q_s.reshape(B, T, SWA_KV_HEADS, SWA_GROUP, SWA_HEAD_DIM)
    k_s = k_s.reshape(B, T, SWA_KV_HEADS, SWA_HEAD_DIM)
    v_s = v_s.reshape(B, T, SWA_KV_HEADS, SWA_HEAD_DIM)
    if k_buf is None:
        n = T // WINDOW
        zpad = jnp.zeros((B, WINDOW, SWA_KV_HEADS, SWA_HEAD_DIM), k_s.dtype)
        kb = jnp.concatenate([zpad, k_s], axis=1).reshape(B, n + 1, WINDOW, SWA_KV_HEADS, SWA_HEAD_DIM)
        vb = jnp.concatenate([zpad, v_s], axis=1).reshape(B, n + 1, WINDOW, SWA_KV_HEADS, SWA_HEAD_DIM)
        keys = jnp.concatenate([kb[:, :-1], kb[:, 1:]], axis=2)
        vals = jnp.concatenate([vb[:, :-1], vb[:, 1:]], axis=2)
        blk = jnp.arange(n, dtype=jnp.int32)[:, None]
        q_pos = blk * WINDOW + jnp.arange(WINDOW, dtype=jnp.int32)[None]
        k_pos = (blk - 1) * WINDOW + jnp.arange(2 * WINDOW, dtype=jnp.int32)[None]
        o_s = swa_block_attend(q_s.reshape(B, n, WINDOW, SWA_KV_HEADS, SWA_GROUP, SWA_HEAD_DIM),
                               keys, vals, q_pos, k_pos, swa_sinks)
        k_new = k_s[:, T - WINDOW:]
        v_new = v_s[:, T - WINDOW:]
    else:
        keys = jnp.concatenate([k_buf.astype(k_s.dtype), k_s], axis=1)
        vals = jnp.concatenate([v_buf.astype(v_s.dtype), v_s], axis=1)
        q_pos = (PAST_LEN + jnp.arange(T, dtype=jnp.int32))[None]
        k_pos = (PAST_LEN - WINDOW + jnp.arange(WINDOW + T, dtype=jnp.int32))[None]
        o_s = swa_block_attend(q_s[:, None], keys[:, None], vals[:, None], q_pos, k_pos, swa_sinks)
        k_new = keys[:, T:]
        v_new = vals[:, T:]
    o_s = o_s.reshape(B, T, SWA_HEADS * SWA_HEAD_DIM)
    u_swa = o_s @ w_br_swa

    merged = jax.nn.sigmoid(gate_gla) * u_gla + jax.nn.sigmoid(gate_swa) * u_swa
    return merged @ w_o, gla_state, k_new, v_new


def decoder_layer(x, gla_s0, k_buf, v_buf, p):
    x = layer_norm(ALPHA * x + 0.5 * swiglu_ffn(x, p['w_ffn1_in'], p['w_ffn1_out']), p['ln_ffn1_w'], p['ln_ffn1_b'])
    mix, s_new, k_new, v_new = token_mixers(x, gla_s0, k_buf, v_buf, p['w_in'], p['b_in'], p['w_gla_gate'],
                                            p['b_gla_gate'], p['gla_norm_w'], p['swa_sinks'],
                                            p['w_br_gla'], p['w_br_swa'], p['w_o'])
    x = layer_norm(ALPHA * x + mix, p['ln_mix_w'], p['ln_mix_b'])
    x = layer_norm(ALPHA * x + 0.5 * swiglu_ffn(x, p['w_ffn2_in'], p['w_ffn2_out']), p['ln_ffn2_w'], p['ln_ffn2_b'])
    return x, s_new, k_new, v_new


def setup_inputs(seed: int = 0) -> dict:
    key = jax.random.key(seed)
    ks = jax.random.split(key, 24)
    L, D = DEPTH, D_MODEL

    def nrm(k, shape, scale):
        return jax.random.normal(k, shape, jnp.float32) * scale

    return {
        'x_prompt': nrm(ks[0], (BATCH, SEQ, D), 1.0),
        'x_sample': nrm(ks[1], (DEC_BATCH, DEC_SEQ, D), 1.0),
        'state_gla': nrm(ks[2], (L, DEC_BATCH, GLA_HEADS, GLA_DK, GLA_DV), 0.5),
        'cache_swa_k': nrm(ks[3], (L, DEC_BATCH, WINDOW, SWA_KV_HEADS, SWA_HEAD_DIM), 1.0),
        'cache_swa_v': nrm(ks[4], (L, DEC_BATCH, WINDOW, SWA_KV_HEADS, SWA_HEAD_DIM), 1.0),
        'w_ffn1_in': nrm(ks[5], (L, D, 2 * D_FF), D ** -0.5),
        'w_ffn1_out': nrm(ks[6], (L, D_FF, D), BETA * D_FF ** -0.5),
        'ln_ffn1_w': 1.0 + nrm(ks[7], (L, D), 0.02),
        'ln_ffn1_b': nrm(ks[8], (L, D), 0.02),
        'w_in': nrm(ks[9], (L, D, D_IN), D ** -0.5),
        'b_in': nrm(ks[10], (L, D_IN), 0.02),
        'w_gla_gate': nrm(ks[11], (L, GLA_RANK, GLA_HEADS * GLA_DK), GLA_RANK ** -0.5),
        'b_gla_gate': nrm(ks[12], (L, GLA_HEADS * GLA_DK), 0.1),
        'gla_norm_w': 1.0 + nrm(ks[13], (L, GLA_DV), 0.02),
        'swa_sinks': nrm(ks[14], (L, SWA_HEADS), 0.5),
        'w_br_gla': nrm(ks[15], (L, GLA_HEADS * GLA_DV, D), BETA * (GLA_HEADS * GLA_DV) ** -0.5),
        'w_br_swa': nrm(ks[16], (L, SWA_HEADS * SWA_HEAD_DIM, D), BETA * (SWA_HEADS * SWA_HEAD_DIM) ** -0.5),
        'w_o': nrm(ks[17], (L, D, D), BETA * D ** -0.5),
        'ln_mix_w': 1.0 + nrm(ks[18], (L, D), 0.02),
        'ln_mix_b': nrm(ks[19], (L, D), 0.02),
        'w_ffn2_in': nrm(ks[20], (L, D, 2 * D_FF), D ** -0.5),
        'w_ffn2_out': nrm(ks[21], (L, D_FF, D), BETA * D_FF ** -0.5),
        'ln_ffn2_w': 1.0 + nrm(ks[22], (L, D), 0.02),
        'ln_ffn2_b': nrm(ks[23], (L, D), 0.02),
    }


def reference(x_prompt, x_sample, state_gla, cache_swa_k, cache_swa_v, w_ffn1_in, w_ffn1_out, ln_ffn1_w,
              ln_ffn1_b, w_in, b_in, w_gla_gate, b_gla_gate, gla_norm_w, swa_sinks, w_br_gla, w_br_swa, w_o,
              ln_mix_w, ln_mix_b, w_ffn2_in, w_ffn2_out, ln_ffn2_w, ln_ffn2_b):
    hp, hs = x_prompt, x_sample
    sp_list, kp_list, vp_list, ss_list, ks_list, vs_list = [], [], [], [], [], []
    for l in range(DEPTH):
        p = {'w_ffn1_in': w_ffn1_in[l], 'w_ffn1_out': w_ffn1_out[l], 'ln_ffn1_w': ln_ffn1_w[l],
             'ln_ffn1_b': ln_ffn1_b[l], 'w_in': w_in[l], 'b_in': b_in[l], 'w_gla_gate': w_gla_gate[l],
             'b_gla_gate': b_gla_gate[l], 'gla_norm_w': gla_norm_w[l], 'swa_sinks': swa_sinks[l],
             'w_br_gla': w_br_gla[l], 'w_br_swa': w_br_swa[l], 'w_o': w_o[l], 'ln_mix_w': ln_mix_w[l],
             'ln_mix_b': ln_mix_b[l], 'w_ffn2_in': w_ffn2_in[l], 'w_ffn2_out': w_ffn2_out[l],
             'ln_ffn2_w': ln_ffn2_w[l], 'ln_ffn2_b': ln_ffn2_b[l]}
        hp, sp, kp, vp = decoder_layer(hp, None, None, None, p)
        hs, ss, ksn, vsn = decoder_layer(hs, state_gla[l], cache_swa_k[l], cache_swa_v[l], p)
        sp_list.append(sp); kp_list.append(kp); vp_list.append(vp)
        ss_list.append(ss); ks_list.append(ksn); vs_list.append(vsn)
    return (hp, hs, jnp.stack(sp_list), jnp.stack(kp_list), jnp.stack(vp_list),
            jnp.stack(ss_list), jnp.stack(ks_list), jnp.stack(vs_list))
```

```python
import functools

import jax
import jax.numpy as jnp
from jax import lax
from jax.experimental import pallas as pl
from jax.experimental.pallas import tpu as pltpu

F32 = jnp.float32
BF16 = jnp.bfloat16

D_MODEL = 2048
SEQ = 8192
DEC_BATCH = 128
DEC_SEQ = 8
N_PROMPT = SEQ
N_SAMPLE = DEC_BATCH * DEC_SEQ
N_TOK = N_PROMPT + N_SAMPLE
PAST_LEN = 16384

GLA_HEADS = 4
GLA_DK = 256
GLA_DV = 512
GLA_RANK = 16
GLA_GATE_NORM = 16.0
GLA_CHUNK = 64
SWA_HEAD_DIM = 64
SWA_HEADS = 32
SWA_KV_HEADS = 8
SWA_GROUP = 4
WINDOW = 128
D_FF = 5632
ALPHA = 2.0 ** 0.25
LN_EPS = 1e-5
RMS_EPS = 1e-6
NEG_INF = -1e30

LANES = 128
SUBLANES = 8
VMEM_LIMIT_BYTES = 56 * 1024 * 1024

OFF_QS = 0
OFF_GATE_GLA = 2048
OFF_GATE_SWA = 4096
OFF_GG = 6144
OFF_VG = 8192
OFF_QG = 10240
OFF_KG = 11264
OFF_KS = 12288
OFF_VS = 12800
D_Z = 13312
EXP_CLAMP = 80.0


def _dot(a, b):
    return jnp.dot(a, b, preferred_element_type=F32)


def _dot_nt(a, b):
    return lax.dot_general(a, b, (((1,), (1,)), ((), ())), preferred_element_type=F32)


def _sigmoid(x):
    return 1.0 / (1.0 + jnp.exp(-x))


def _layer_norm(y, w, b):
    mu = jnp.mean(y, axis=-1, keepdims=True)
    yc = y - mu
    var = jnp.mean(yc * yc, axis=-1, keepdims=True)
    return yc * lax.rsqrt(var + LN_EPS) * w + b


def _params(*sem):
    return pltpu.CompilerParams(dimension_semantics=sem, vmem_limit_bytes=VMEM_LIMIT_BYTES)


FFN_TM = 512
FFN_TF = 512


def _ffn_ln_kernel(x_ref, wg_ref, wu_ref, wo_ref, lnw_ref, lnb_ref, o_ref, xb_ref, acc_ref):
    j = pl.program_id(1)

    @pl.when(j == 0)
    def _():
        xb_ref[...] = x_ref[...].astype(BF16)
        acc_ref[...] = jnp.zeros_like(acc_ref)

    xb = xb_ref[...]
    g = _dot(xb, wg_ref[...])
    u = _dot(xb, wu_ref[...])
    a = (g * _sigmoid(g) * u).astype(BF16)
    acc_ref[...] += _dot(a, wo_ref[...])

    @pl.when(j == pl.num_programs(1) - 1)
    def _():
        y = ALPHA * x_ref[...] + 0.5 * acc_ref[...]
        o_ref[...] = _layer_norm(y, lnw_ref[...], lnb_ref[...])


def _ffn_ln(x, w_in, w_out, ln_w, ln_b):
    n = x.shape[0]
    nj = D_FF // FFN_TF
    return pl.pallas_call(
        _ffn_ln_kernel,
        grid=(n // FFN_TM, nj),
        in_specs=[
            pl.BlockSpec((FFN_TM, D_MODEL), lambda i, j: (i, 0)),
            pl.BlockSpec((D_MODEL, FFN_TF), lambda i, j: (0, j)),
            pl.BlockSpec((D_MODEL, FFN_TF), lambda i, j: (0, j + nj)),
            pl.BlockSpec((FFN_TF, D_MODEL), lambda i, j: (j, 0)),
            pl.BlockSpec((1, D_MODEL), lambda i, j: (0, 0)),
            pl.BlockSpec((1, D_MODEL), lambda i, j: (0, 0)),
        ],
        out_specs=pl.BlockSpec((FFN_TM, D_MODEL), lambda i, j: (i, 0)),
        out_shape=jax.ShapeDtypeStruct((n, D_MODEL), F32),
        scratch_shapes=[pltpu.VMEM((FFN_TM, D_MODEL), BF16), pltpu.VMEM((FFN_TM, D_MODEL), F32)],
        compiler_params=_params("parallel", "arbitrary"),
        name="ffn_ln",
    )(x, w_in, w_in, w_out, ln_w, ln_b)


PROJ_TM = 1024
PROJ_TN = 512


def _proj_kernel(x_ref, w_ref, b_ref, wr_ref, br_ref, wgate_ref, bgate_ref, z_ref, la_ref, xb_ref):
    j = pl.program_id(1)

    @pl.when(j == 0)
    def _():
        xb = x_ref[...].astype(BF16)
        xb_ref[...] = xb
        r = _dot(xb, wr_ref[...]) + br_ref[...]
        t = _dot(r.astype(BF16), wgate_ref[...]) + bgate_ref[...]
        ls = -(jnp.maximum(-t, 0.0) + jnp.log1p(jnp.exp(-jnp.abs(t))))
        la_ref[...] = ls / GLA_GATE_NORM

    z_ref[...] = _dot(xb_ref[...], w_ref[...]) + b_ref[...]


def _proj(x, w, b, w_r, b_r, w_gate, b_gate):
    n = x.shape[0]
    nk = GLA_HEADS * GLA_DK
    return pl.pallas_call(
        _proj_kernel,
        grid=(n // PROJ_TM, D_Z // PROJ_TN),
        in_specs=[
            pl.BlockSpec((PROJ_TM, D_MODEL), lambda i, j: (i, 0)),
            pl.BlockSpec((D_MODEL, PROJ_TN), lambda i, j: (0, j)),
            pl.BlockSpec((1, PROJ_TN), lambda i, j: (0, j)),
            pl.BlockSpec((D_MODEL, LANES), lambda i, j: (0, 0)),
            pl.BlockSpec((1, LANES), lambda i, j: (0, 0)),
            pl.BlockSpec((LANES, nk), lambda i, j: (0, 0)),
            pl.BlockSpec((1, nk), lambda i, j: (0, 0)),
        ],
        out_specs=[
            pl.BlockSpec((PROJ_TM, PROJ_TN), lambda i, j: (i, j)),
            pl.BlockSpec((PROJ_TM, nk), lambda i, j: (i, 0)),
        ],
        out_shape=[jax.ShapeDtypeStruct((n, D_Z), F32), jax.ShapeDtypeStruct((n, nk), F32)],
        scratch_shapes=[pltpu.VMEM((PROJ_TM, D_MODEL), BF16)],
        compiler_params=_params("parallel", "arbitrary"),
        name="proj",
    )(x, w, b, w_r, b_r, w_gate, b_gate)


def _split_hi_lo(x):
    hi = x.astype(BF16)
    lo = (x - hi.astype(F32)).astype(BF16)
    return jnp.concatenate([hi, lo], axis=1)


def _segment_cumsum(la, seg):
    c = la.shape[0]
    row = lax.broadcasted_iota(jnp.int32, (c, c), 0)
    col = lax.broadcasted_iota(jnp.int32, (c, c), 1)
    tri = (row >= col) & ((row // seg) == (col // seg))
    bb = _dot(jnp.where(tri, 1.0, 0.0).astype(BF16), _split_hi_lo(la))
    dk = la.shape[1]
    return bb[:, :dk] + bb[:, dk:]


def _row_of_block(b, blk, r):
    c, dk = b.shape
    if blk == c:
        return jnp.broadcast_to(b[r:r + 1, :], (c, dk))
    b3 = b.reshape(c // blk, blk, dk)
    return jnp.broadcast_to(b3[:, r:r + 1, :], (c // blk, blk, dk)).reshape(c, dk)


def _intra_scores(q, k, b, seg):
    c = q.shape[0]
    t = lax.broadcasted_iota(jnp.int32, (c, c), 0)
    s = lax.broadcasted_iota(jnp.int32, (c, c), 1)
    blk = SUBLANES
    r = _row_of_block(b, blk, 3)
    qa = (q * jnp.exp(jnp.minimum(b - r, EXP_CLAMP))).astype(BF16)
    ka = (k * jnp.exp(jnp.minimum(r - b, EXP_CLAMP))).astype(BF16)
    scores = jnp.where(((t // blk) == (s // blk)) & (s <= t), _dot_nt(qa, ka), 0.0)
    h = blk
    while 2 * h <= seg:
        r = _row_of_block(b, 2 * h, h - 1)
        qa = (q * jnp.exp(jnp.minimum(b - r, 0.0))).astype(BF16)
        ka = (k * jnp.exp(jnp.minimum(r - b, 0.0))).astype(BF16)
        mask = ((t // (2 * h)) == (s // (2 * h))) & (((t // h) % 2) == 1) & (((s // h) % 2) == 0)
        scores = jnp.where(mask, _dot_nt(qa, ka), scores)
        h *= 2
    return scores


def _gla_out(o, g, normw):
    ms = jnp.mean(o * o, axis=-1, keepdims=True)
    of = o * lax.rsqrt(ms + RMS_EPS) * normw
    return (of * (g * _sigmoid(g))).astype(BF16)


GLA_TOK = 256


def _gla_prompt_kernel(q_ref, k_ref, v_ref, g_ref, la_ref, normw_ref, o_ref, s_ref):
    @pl.when(pl.program_id(0) == 0)
    def _():
        s_ref[...] = jnp.zeros_like(s_ref)

    normw = normw_ref[...]
    scale = GLA_DK ** -0.5
    for cc in range(GLA_TOK // GLA_CHUNK):
        rows = slice(cc * GLA_CHUNK, (cc + 1) * GLA_CHUNK)
        for h in range(GLA_HEADS):
            kcols = slice(h * GLA_DK, (h + 1) * GLA_DK)
            vcols = slice(h * GLA_DV, (h + 1) * GLA_DV)
            la = la_ref[rows, kcols]
            q = q_ref[rows, kcols] * scale
            k = k_ref[rows, kcols]
            vb = v_ref[rows, vcols].astype(BF16)
            b = _segment_cumsum(la, GLA_CHUNK)
            bl = b[GLA_CHUNK - 1:GLA_CHUNK, :]
            scores = _intra_scores(q, k, b, GLA_CHUNK)
            state = s_ref[0, 0, h]
            qt = (q * jnp.exp(b)).astype(BF16)
            o = _dot(qt, state.astype(BF16)) + _dot(scores.astype(BF16), vb)
            kd_t = (k * jnp.exp(bl - b)).T.astype(BF16)
            decay = jnp.broadcast_to(jnp.exp(bl), (SUBLANES, GLA_DK)).T[:, 0:1]
            s_ref[0, 0, h] = decay * state + _dot(kd_t, vb)
            o_ref[rows, vcols] = _gla_out(o, g_ref[rows, vcols], normw)


def _gla_prompt(z, la, norm_w):
    kb = GLA_HEADS * GLA_DK
    vb = GLA_HEADS * GLA_DV
    return pl.pallas_call(
        _gla_prompt_kernel,
        grid=(N_PROMPT // GLA_TOK,),
        in_specs=[
            pl.BlockSpec((GLA_TOK, kb), lambda c: (c, OFF_QG // kb)),
            pl.BlockSpec((GLA_TOK, kb), lambda c: (c, OFF_KG // kb)),
            pl.BlockSpec((GLA_TOK, vb), lambda c: (c, OFF_VG // vb)),
            pl.BlockSpec((GLA_TOK, vb), lambda c: (c, OFF_GG // vb)),
            pl.BlockSpec((GLA_TOK, kb), lambda c: (c, 0)),
            pl.BlockSpec((1, GLA_DV), lambda c: (0, 0)),
        ],
        out_specs=[
            pl.BlockSpec((GLA_TOK, vb), lambda c: (c, 0)),
            pl.BlockSpec((1, 1, GLA_HEADS, GLA_DK, GLA_DV), lambda c: (0, 0, 0, 0, 0)),
        ],
        out_shape=[
            jax.ShapeDtypeStruct((N_PROMPT, vb), BF16),
            jax.ShapeDtypeStruct((1, 1, GLA_HEADS, GLA_DK, GLA_DV), F32),
        ],
        compiler_params=_params("arbitrary"),
        name="gla_prompt",
    )(z, z, z, z, la, norm_w)


GLA_SEQS = GLA_CHUNK // DEC_SEQ


def _gla_sample_kernel(q_ref, k_ref, v_ref, g_ref, la_ref, normw_ref, s0_ref, o_ref, s_ref):
    c = GLA_CHUNK
    la = la_ref[...]
    q = q_ref[...] * (GLA_DK ** -0.5)
    k = k_ref[...]
    vb = v_ref[...].astype(BF16)
    b = _segment_cumsum(la, DEC_SEQ)
    bl = _row_of_block(b, DEC_SEQ, DEC_SEQ - 1)
    scores = _intra_scores(q, k, b, DEC_SEQ)
    o = _dot(scores.astype(BF16), vb)
    qt = q * jnp.exp(b)
    kd_t = (k * jnp.exp(bl - b)).T
    decay_t = jnp.exp(bl).T
    row_seq = lax.broadcasted_iota(jnp.int32, (c, GLA_DK), 0) // DEC_SEQ
    col_seq = lax.broadcasted_iota(jnp.int32, (GLA_DK, c), 1) // DEC_SEQ
    for s in range(GLA_SEQS):
        state = s0_ref[0, s, 0]
        o = o + _dot(jnp.where(row_seq == s, qt, 0.0).astype(BF16), state.astype(BF16))
        upd = _dot(jnp.where(col_seq == s, kd_t, 0.0).astype(BF16), vb)
        s_ref[0, s, 0] = decay_t[:, s * DEC_SEQ:s * DEC_SEQ + 1] * state + upd
    o_ref[...] = _gla_out(o, g_ref[...], normw_ref[...])


def _gla_sample(z, la, norm_w, state):
    r0 = N_PROMPT // GLA_CHUNK
    sblk = (1, GLA_SEQS, 1, GLA_DK, GLA_DV)
    return pl.pallas_call(
        _gla_sample_kernel,
        grid=(DEC_BATCH // GLA_SEQS, GLA_HEADS),
        in_specs=[
            pl.BlockSpec((GLA_CHUNK, GLA_DK), lambda i, h: (r0 + i, OFF_QG // GLA_DK + h)),
            pl.BlockSpec((GLA_CHUNK, GLA_DK), lambda i, h: (r0 + i, OFF_KG // GLA_DK + h)),
            pl.BlockSpec((GLA_CHUNK, GLA_DV), lambda i, h: (r0 + i, OFF_VG // GLA_DV + h)),
            pl.BlockSpec((GLA_CHUNK, GLA_DV), lambda i, h: (r0 + i, OFF_GG // GLA_DV + h)),
            pl.BlockSpec((GLA_CHUNK, GLA_DK), lambda i, h: (r0 + i, h)),
            pl.BlockSpec((1, GLA_DV), lambda i, h: (0, 0)),
            pl.BlockSpec(sblk, lambda i, h: (0, i, h, 0, 0)),
        ],
        out_specs=[
            pl.BlockSpec((GLA_CHUNK, GLA_DV), lambda i, h: (i, h)),
            pl.BlockSpec(sblk, lambda i, h: (0, i, h, 0, 0)),
        ],
        out_shape=[
            jax.ShapeDtypeStruct((N_SAMPLE, GLA_HEADS * GLA_DV), BF16),
            jax.ShapeDtypeStruct(state.shape, F32),
        ],
        compiler_params=_params("parallel", "parallel"),
        name="gla_sample",
    )(z, z, z, z, la, norm_w, state)


def _alibi_slope(head):
    return 2.0 ** (-8.0 * (head + 1) / SWA_HEADS)


def _head_place(head):
    return head // 2, head % 2


def _swa_prompt_kernel(sink_ref, q_ref, kp_ref, kc_ref, vp_ref, vc_ref, o_ref):
    n = pl.program_id(0)
    w = WINDOW
    iq = lax.broadcasted_iota(jnp.int32, (w, 2 * w), 0)
    jk = lax.broadcasted_iota(jnp.int32, (w, 2 * w), 1)
    dist = w + iq - jk
    valid = (dist >= 0) & (dist < w) & ((jk >= w) | (n > 0))
    distf = dist.astype(F32)
    lane_half = lax.broadcasted_iota(jnp.int32, (w, LANES), 1) // SWA_HEAD_DIM
    for p in range(SWA_KV_HEADS // 2):
        cols = slice(p * LANES, (p + 1) * LANES)
        kk = jnp.concatenate([kp_ref[:, cols], kc_ref[:, cols]], axis=0).astype(BF16)
        vv = jnp.concatenate([vp_ref[:, cols], vc_ref[:, cols]], axis=0).astype(BF16)
        for e in range(2):
            for gp in range(SWA_GROUP // 2):
                outs = []
                for g in (2 * gp, 2 * gp + 1):
                    head = (2 * p + e) * SWA_GROUP + g
                    cq, hq = _head_place(head)
                    qc = q_ref[:, cq * LANES:(cq + 1) * LANES]
                    if hq != e:
                        qc = pltpu.roll(qc, SWA_HEAD_DIM, axis=1)
                    qh = jnp.where(lane_half == e, qc, 0.0).astype(BF16)
                    s = _dot_nt(qh, kk) * (SWA_HEAD_DIM ** -0.5) - _alibi_slope(head) * distf
                    s = jnp.where(valid, s, NEG_INF)
                    sink = sink_ref[0, head]
                    m = jnp.maximum(jnp.max(s, axis=1, keepdims=True), sink)
                    pe = jnp.exp(s - m)
                    denom = jnp.sum(pe, axis=1, keepdims=True) + jnp.exp(sink - m)
                    o = _dot((pe / denom).astype(BF16), vv)
                    if hq != e:
                        o = pltpu.roll(o, SWA_HEAD_DIM, axis=1)
                    outs.append(o)
                cq, _ = _head_place((2 * p + e) * SWA_GROUP + 2 * gp)
                o_ref[:, cq * LANES:(cq + 1) * LANES] = jnp.where(lane_half == 0, outs[0], outs[1]).astype(BF16)


def _swa_prompt(z, sinks):
    kvw = SWA_KV_HEADS * SWA_HEAD_DIM
    prev = lambda n: jnp.maximum(n - 1, 0)
    return pl.pallas_call(
        _swa_prompt_kernel,
        grid=(N_PROMPT // WINDOW,),
        in_specs=[
            pl.BlockSpec(memory_space=pltpu.SMEM),
            pl.BlockSpec((WINDOW, D_MODEL), lambda n: (n, OFF_QS // D_MODEL)),
            pl.BlockSpec((WINDOW, kvw), lambda n: (prev(n), OFF_KS // kvw)),
            pl.BlockSpec((WINDOW, kvw), lambda n: (n, OFF_KS // kvw)),
            pl.BlockSpec((WINDOW, kvw), lambda n: (prev(n), OFF_VS // kvw)),
            pl.BlockSpec((WINDOW, kvw), lambda n: (n, OFF_VS // kvw)),
        ],
        out_specs=pl.BlockSpec((WINDOW, D_MODEL), lambda n: (n, 0)),
        out_shape=jax.ShapeDtypeStruct((N_PROMPT, D_MODEL), BF16),
        compiler_params=_params("parallel"),
        name="swa_prompt",
    )(sinks, z, z, z, z, z)


SWA_SEQS = 8
SWA_PAIR_HEADS = 2 * SWA_GROUP


def _swa_sample_kernel(sink_ref, q_ref, kn_ref, vn_ref, ck_ref, cv_ref, o_ref, ok_ref, ov_ref):
    w = WINDOW
    t_tok = DEC_SEQ
    rows = SWA_PAIR_HEADS * t_tok
    row = lax.broadcasted_iota(jnp.int32, (rows, LANES), 0)
    j = lax.broadcasted_iota(jnp.int32, (rows, LANES), 1)
    t = row % t_tok
    hh = row // t_tok
    dist_c = (t + w - j).astype(F32)
    valid_c = j > t
    dist_n = (t - j).astype(F32)
    valid_n = j <= t
    lane_half8 = lax.broadcasted_iota(jnp.int32, (t_tok, LANES), 1) // SWA_HEAD_DIM
    hh1 = hh[:, 0:1]
    zpad = jnp.zeros((w - t_tok, LANES), F32)
    for sq in range(SWA_SEQS):
        tok = slice(sq * t_tok, (sq + 1) * t_tok)
        ck = ck_ref[sq]
        cv = cv_ref[sq]
        kn = kn_ref[tok, :]
        vn = vn_ref[tok, :]
        ok_ref[sq, 0:w - t_tok, :] = ck[t_tok:w, :]
        ok_ref[sq, w - t_tok:w, :] = kn
        ov_ref[sq, 0:w - t_tok, :] = cv[t_tok:w, :]
        ov_ref[sq, w - t_tok:w, :] = vn
        for p in range(SWA_KV_HEADS // 2):
            cols = slice(p * LANES, (p + 1) * LANES)
            kc = ck[:, cols].astype(BF16)
            vc = cv[:, cols].astype(BF16)
            knp = jnp.concatenate([kn[:, cols], zpad], axis=0).astype(BF16)
            vnp = jnp.concatenate([vn[:, cols], zpad], axis=0).astype(BF16)
            pieces = []
            slope = jnp.zeros((rows, 1), F32)
            sink = jnp.zeros((rows, 1), F32)
            for e in range(2):
                for g in range(SWA_GROUP):
                    head = (2 * p + e) * SWA_GROUP + g
                    cq, hq = _head_place(head)
                    qc = q_ref[tok, cq * LANES:(cq + 1) * LANES]
                    if hq != e:
                        qc = pltpu.roll(qc, SWA_HEAD_DIM, axis=1)
                    pieces.append(jnp.where(lane_half8 == e, qc, 0.0))
                    sel = hh1 == (e * SWA_GROUP + g)
                    slope = jnp.where(sel, _alibi_slope(head), slope)
                    sink = jnp.where(sel, sink_ref[0, head], sink)
            qh = jnp.concatenate(pieces, axis=0).astype(BF16)
            scale = SWA_HEAD_DIM ** -0.5
            s_c = jnp.where(valid_c, _dot_nt(qh, kc) * scale - slope * dist_c, NEG_INF)
            s_n = jnp.where(valid_n, _dot_nt(qh, knp) * scale - slope * dist_n, NEG_INF)
            m = jnp.maximum(jnp.maximum(jnp.max(s_c, axis=1, keepdims=True),
                                        jnp.max(s_n, axis=1, keepdims=True)), sink)
            pc = jnp.exp(s_c - m)
            pn = jnp.exp(s_n - m)
            denom = (jnp.sum(pc, axis=1, keepdims=True) + jnp.sum(pn, axis=1, keepdims=True)
                     + jnp.exp(sink - m))
            o = _dot((pc / denom).astype(BF16), vc) + _dot((pn / denom).astype(BF16), vnp)
            for e in range(2):
                for gp in range(SWA_GROUP // 2):
                    outs = []
                    for g in (2 * gp, 2 * gp + 1):
                        hidx = e * SWA_GROUP + g
                        _, hq = _head_place((2 * p + e) * SWA_GROUP + g)
                        piece = o[hidx * t_tok:(hidx + 1) * t_tok, :]
                        if hq != e:
                            piece = pltpu.roll(piece, SWA_HEAD_DIM, axis=1)
                        outs.append(piece)
                    cq, _ = _head_place((2 * p + e) * SWA_GROUP + 2 * gp)
                    o_ref[tok, cq * LANES:(cq + 1) * LANES] = jnp.where(
                        lane_half8 == 0, outs[0], outs[1]).astype(BF16)


def _swa_sample(z, sinks, cache_k, cache_v):
    kvw = SWA_KV_HEADS * SWA_HEAD_DIM
    tok = SWA_SEQS * DEC_SEQ
    r0 = N_PROMPT // tok
    cblk = (SWA_SEQS, WINDOW, kvw)
    cshape = jax.ShapeDtypeStruct((DEC_BATCH, WINDOW, kvw), F32)
    return pl.pallas_call(
        _swa_sample_kernel,
        grid=(DEC_BATCH // SWA_SEQS,),
        in_specs=[
            pl.BlockSpec(memory_space=pltpu.SMEM),
            pl.BlockSpec((tok, D_MODEL), lambda i: (r0 + i, OFF_QS // D_MODEL)),
            pl.BlockSpec((tok, kvw), lambda i: (r0 + i, OFF_KS // kvw)),
            pl.BlockSpec((tok, kvw), lambda i: (r0 + i, OFF_VS // kvw)),
            pl.BlockSpec(cblk, lambda i: (i, 0, 0)),
            pl.BlockSpec(cblk, lambda i: (i, 0, 0)),
        ],
        out_specs=[
            pl.BlockSpec((tok, D_MODEL), lambda i: (i, 0)),
            pl.BlockSpec(cblk, lambda i: (i, 0, 0)),
            pl.BlockSpec(cblk, lambda i: (i, 0, 0)),
        ],
        out_shape=[jax.ShapeDtypeStruct((N_SAMPLE, D_MODEL), BF16), cshape, cshape],
        compiler_params=_params("parallel"),
        name="swa_sample",
    )(sinks, z, z, z, cache_k, cache_v)


MERGE_TM = 512
MERGE_TN = 256


def _merge_ln_kernel(og_ref, os_ref, wbg_ref, wbs_ref, gg_ref, gs_ref, wo_ref, x_ref, lnw_ref, lnb_ref,
                     o_ref, acc_ref):
    j = pl.program_id(1)

    @pl.when(j == 0)
    def _():
        acc_ref[...] = jnp.zeros_like(acc_ref)

    u_gla = _dot(og_ref[...], wbg_ref[...])
    u_swa = _dot(os_ref[...], wbs_ref[...])
    merged = _sigmoid(gg_ref[...]) * u_gla + _sigmoid(gs_ref[...]) * u_swa
    acc_ref[...] += _dot(merged.astype(BF16), wo_ref[...])

    @pl.when(j == pl.num_programs(1) - 1)
    def _():
        y = ALPHA * x_ref[...] + acc_ref[...]
        o_ref[...] = _layer_norm(y, lnw_ref[...], lnb_ref[...])


def _merge_ln(og, osw, w_bg, w_bs, z, w_o, x, ln_w, ln_b):
    n = x.shape[0]
    return pl.pallas_call(
        _merge_ln_kernel,
        grid=(n // MERGE_TM, D_MODEL // MERGE_TN),
        in_specs=[
            pl.BlockSpec((MERGE_TM, D_MODEL), lambda i, j: (i, 0)),
            pl.BlockSpec((MERGE_TM, D_MODEL), lambda i, j: (i, 0)),
            pl.BlockSpec((D_MODEL, MERGE_TN), lambda i, j: (0, j)),
            pl.BlockSpec((D_MODEL, MERGE_TN), lambda i, j: (0, j)),
            pl.BlockSpec((MERGE_TM, MERGE_TN), lambda i, j: (i, OFF_GATE_GLA // MERGE_TN + j)),
            pl.BlockSpec((MERGE_TM, MERGE_TN), lambda i, j: (i, OFF_GATE_SWA // MERGE_TN + j)),
            pl.BlockSpec((MERGE_TN, D_MODEL), lambda i, j: (j, 0)),
            pl.BlockSpec((MERGE_TM, D_MODEL), lambda i, j: (i, 0)),
            pl.BlockSpec((1, D_MODEL), lambda i, j: (0, 0)),
            pl.BlockSpec((1, D_MODEL), lambda i, j: (0, 0)),
        ],
        out_specs=pl.BlockSpec((MERGE_TM, D_MODEL), lambda i, j: (i, 0)),
        out_shape=jax.ShapeDtypeStruct((n, D_MODEL), F32),
        scratch_shapes=[pltpu.VMEM((MERGE_TM, D_MODEL), F32)],
        compiler_params=_params("parallel", "arbitrary"),
        name="merge_ln",
    )(og, osw, w_bg, w_bs, z, z, w_o, x, ln_w, ln_b)


def _regroup_cols(a):
    qg, kg, vg, gg, rg, qs, ks, vs, ga, gs = jnp.split(
        a, [1024, 2048, 4096, 6144, 6160, 8208, 8720, 9232, 11280], axis=-1)
    return jnp.concatenate([qs, ga, gs, gg, vg, qg, kg, ks, vs], axis=-1), rg


def kernel(x_prompt, x_sample, state_gla, cache_swa_k, cache_swa_v, w_ffn1_in, w_ffn1_out, ln_ffn1_w,
           ln_ffn1_b, w_in, b_in, w_gla_gate, b_gla_gate, gla_norm_w, swa_sinks, w_br_gla, w_br_swa, w_o,
           ln_mix_w, ln_mix_b, w_ffn2_in, w_ffn2_out, ln_ffn2_w, ln_ffn2_b):
    assert w_in.shape[0] == 1, "single layer"
    x = jnp.concatenate([x_prompt.reshape(N_PROMPT, D_MODEL), x_sample.reshape(N_SAMPLE, D_MODEL)], axis=0)

    w_main, w_r = _regroup_cols(w_in[0])
    b_main, b_r = _regroup_cols(b_in)
    pad_r = LANES - GLA_RANK
    w_r = jnp.pad(w_r, ((0, 0), (0, pad_r))).astype(BF16)
    b_r = jnp.pad(b_r, ((0, 0), (0, pad_r)))
    w_gate = jnp.pad(w_gla_gate[0], ((0, pad_r), (0, 0))).astype(BF16)

    x1 = _ffn_ln(x, w_ffn1_in[0].astype(BF16), w_ffn1_out[0].astype(BF16), ln_ffn1_w, ln_ffn1_b)
    z, la = _proj(x1, w_main.astype(BF16), b_main, w_r, b_r, w_gate, b_gla_gate)

    og_p, state_p = _gla_prompt(z, la, gla_norm_w)
    og_s, state_s = _gla_sample(z, la, gla_norm_w, state_gla)
    kvw = SWA_KV_HEADS * SWA_HEAD_DIM
    os_p = _swa_prompt(z, swa_sinks)
    os_s, ck_s, cv_s = _swa_sample(z, swa_sinks, cache_swa_k.reshape(DEC_BATCH, WINDOW, kvw),
                                   cache_swa_v.reshape(DEC_BATCH, WINDOW, kvw))
    og = jnp.concatenate([og_p, og_s], axis=0)
    osw = jnp.concatenate([os_p, os_s], axis=0)

    x2 = _merge_ln(og, osw, w_br_gla[0].astype(BF16), w_br_swa[0].astype(BF16), z, w_o[0].astype(BF16),
                   x1, ln_mix_w, ln_mix_b)
    y = _ffn_ln(x2, w_ffn2_in[0].astype(BF16), w_ffn2_out[0].astype(BF16), ln_ffn2_w, ln_ffn2_b)

    cache_shape = (1, 1, WINDOW, SWA_KV_HEADS, SWA_HEAD_DIM)
    k_last = z[N_PROMPT - WINDOW:N_PROMPT, OFF_KS:OFF_KS + kvw].reshape(cache_shape)
    v_last = z[N_PROMPT - WINDOW:N_PROMPT, OFF_VS:OFF_VS + kvw].reshape(cache_shape)
    dec_cache_shape = (1, DEC_BATCH, WINDOW, SWA_KV_HEADS, SWA_HEAD_DIM)
    return (y[:N_PROMPT].reshape(1, SEQ, D_MODEL),
            y[N_PROMPT:].reshape(DEC_BATCH, DEC_SEQ, D_MODEL),
            state_p,
            k_last,
            v_last,
            state_s,
            ck_s.reshape(dec_cache_shape),
            cv_s.reshape(dec_cache_shape))
```
